```python
import math
import jax, jax.numpy as jnp
from jax import lax
import numpy as np

D_MODEL = 1024
BATCH = 8
SEQ = 4096
DEPTH = 4

D_RNN = D_MODEL
RG_HEADS = 4
RG_BLOCK = D_RNN // RG_HEADS
RG_CONV_WIDTH = 4
RG_C = 8.0
D_CONV = D_MODEL
CONV_KERNEL = 31
D_FF = D_MODEL * 7 // 2
N_EXPERTS = 8
TOP_K = 2
D_FF_EXPERT = D_MODEL * 7 // 2
RMS_EPS = 1e-6
LN_EPS = 1e-5

kernel_name = "hybrid_rglru_conformer_moe_trunk"


def rms_norm(x, g):
    x32 = x.astype(jnp.float32)
    y = x32 * lax.rsqrt(jnp.mean(x32 * x32, axis=-1, keepdims=True) + RMS_EPS)
    return (y * g.astype(jnp.float32)).astype(x.dtype)


def layer_norm(x, g, b):
    x32 = x.astype(jnp.float32)
    mu = jnp.mean(x32, axis=-1, keepdims=True)
    xc = x32 - mu
    var = jnp.mean(xc * xc, axis=-1, keepdims=True)
    y = xc * lax.rsqrt(var + LN_EPS) * g.astype(jnp.float32) + b.astype(jnp.float32)
    return y.astype(x.dtype)


def causal_depthwise_conv(x, w, b):
    width, chans = w.shape
    y = lax.conv_general_dilated(
        x, w[:, None, :].astype(x.dtype), window_strides=(1,), padding=[(width - 1, 0)],
        dimension_numbers=("NWC", "WIO", "NWC"), feature_group_count=chans)
    return y + b


def rg_lru(x, gate_a_w, gate_a_b, gate_x_w, gate_x_b, lam):
    bsz, seq, ch = x.shape
    xh = x.reshape(bsz, seq, RG_HEADS, RG_BLOCK)
    r = jax.nn.sigmoid((jnp.einsum("bshi,hij->bshj", xh, gate_a_w) + gate_a_b).astype(jnp.float32))
    i = jax.nn.sigmoid((jnp.einsum("bshi,hij->bshj", xh, gate_x_w) + gate_x_b).astype(jnp.float32))
    r = r.reshape(bsz, seq, ch)
    i = i.reshape(bsz, seq, ch)
    log_a = -RG_C * r * jax.nn.softplus(-lam.astype(jnp.float32))
    a = jnp.exp(log_a)
    mult = jnp.sqrt(-jnp.expm1(2.0 * log_a))
    u = mult * (i * x.astype(jnp.float32))

    def combine(left, right):
        a1, b1 = left
        a2, b2 = right
        return a1 * a2, a2 * b1 + b2

    _, h = lax.associative_scan(combine, (a, u), axis=1)
    return h.astype(x.dtype)


def recurrent_block(h, w_in, conv_w, conv_b, ga_w, ga_b, gx_w, gx_b, lam, w_out):
    proj = jnp.einsum("bsd,dn->bsn", h, w_in)
    y_branch, x_branch = jnp.split(proj, 2, axis=-1)
    y_branch = jax.nn.gelu(y_branch, approximate=True)
    x_branch = causal_depthwise_conv(x_branch, conv_w, conv_b)
    x_branch = rg_lru(x_branch, ga_w, ga_b, gx_w, gx_b, lam)
    return jnp.einsum("bsn,nd->bsd", y_branch * x_branch, w_out)


def conformer_conv_module(h, w_pw1, b_pw1, dw_w, dw_b, ln_g, ln_b, w_pw2, b_pw2):
    u = jnp.einsum("bsd,dn->bsn", h, w_pw1) + b_pw1
    val, gate = jnp.split(u, 2, axis=-1)
    u = val * jax.nn.sigmoid(gate)
    u = causal_depthwise_conv(u, dw_w, dw_b)
    u = layer_norm(u, ln_g, ln_b)
    u = jax.nn.silu(u)
    return jnp.einsum("bsn,nd->bsd", u, w_pw2) + b_pw2


def swiglu(h, w_gate_up, w_down):
    gu = h @ w_gate_up
    g, up = jnp.split(gu, 2, axis=-1)
    return (jax.nn.silu(g) * up) @ w_down


def moe_swiglu(h, router, w_gate_up, w_down):
    bsz, seq, d = h.shape
    hf = h.reshape(bsz * seq, d)
    logits = (hf @ router).astype(jnp.float32)
    vals, idx = lax.top_k(logits, TOP_K)
    weights = jax.nn.softmax(vals, axis=-1)
    gates = jnp.sum(jax.nn.one_hot(idx, N_EXPERTS, dtype=jnp.float32) * weights[..., None], axis=1)
    gates = gates.astype(hf.dtype)
    y = jnp.zeros_like(hf)
    for e in range(N_EXPERTS):
        y = y + gates[:, e:e + 1] * swiglu(hf, w_gate_up[e], w_down[e])
    return y.reshape(bsz, seq, d)


def setup_inputs(seed: int = 0) -> dict:
    key = jax.random.key(seed)
    ks = jax.random.split(key, 32)
    f32 = jnp.float32
    n_even = (DEPTH + 1) // 2
    n_odd = DEPTH // 2

    def nrm(k, shape, fan_in):
        return jax.random.normal(k, shape, f32) * (fan_in ** -0.5)

    def small(k, shape, scale=0.02):
        return jax.random.normal(k, shape, f32) * scale

    u = jax.random.uniform(ks[11], (n_even, D_RNN), f32, minval=0.9, maxval=0.999)
    rec_lambda = jnp.log(u) - jnp.log1p(-u)

    return {
        "x": jax.random.normal(ks[0], (BATCH, SEQ, D_MODEL), f32),
        "norm_mix": 1.0 + small(ks[1], (DEPTH, D_MODEL)),
        "norm_ffn": 1.0 + small(ks[2], (DEPTH, D_MODEL)),
        "norm_out": 1.0 + small(ks[3], (D_MODEL,)),
        "rec_w_in": nrm(ks[4], (n_even, D_MODEL, 2 * D_RNN), D_MODEL),
        "rec_conv_w": nrm(ks[5], (n_even, RG_CONV_WIDTH, D_RNN), RG_CONV_WIDTH),
        "rec_conv_b": small(ks[6], (n_even, D_RNN)),
        "rec_gate_a_w": nrm(ks[7], (n_even, RG_HEADS, RG_BLOCK, RG_BLOCK), RG_BLOCK),
        "rec_gate_a_b": small(ks[8], (n_even, RG_HEADS, RG_BLOCK)),
        "rec_gate_x_w": nrm(ks[9], (n_even, RG_HEADS, RG_BLOCK, RG_BLOCK), RG_BLOCK),
        "rec_gate_x_b": small(ks[10], (n_even, RG_HEADS, RG_BLOCK)),
        "rec_lambda": rec_lambda,
        "rec_w_out": nrm(ks[12], (n_even, D_RNN, D_MODEL), D_RNN),
        "ffn_w_gate_up": nrm(ks[13], (n_even, D_MODEL, 2 * D_FF), D_MODEL),
        "ffn_w_down": nrm(ks[14], (n_even, D_FF, D_MODEL), D_FF),
        "cnv_w_pw1": nrm(ks[15], (n_odd, D_MODEL, 2 * D_CONV), D_MODEL),
        "cnv_b_pw1": small(ks[16], (n_odd, 2 * D_CONV)),
        "cnv_dw_w": nrm(ks[17], (n_odd, CONV_KERNEL, D_CONV), CONV_KERNEL),
        "cnv_dw_b": small(ks[18], (n_odd, D_CONV)),
        "cnv_ln_g": 1.0 + small(ks[19], (n_odd, D_CONV)),
        "cnv_ln_b": small(ks[20], (n_odd, D_CONV)),
        "cnv_w_pw2": nrm(ks[21], (n_odd, D_CONV, D_MODEL), D_CONV),
        "cnv_b_pw2": small(ks[22], (n_odd, D_MODEL)),
        "moe_router": nrm(ks[23], (n_odd, D_MODEL, N_EXPERTS), D_MODEL),
        "moe_w_gate_up": nrm(ks[24], (n_odd, N_EXPERTS, D_MODEL, 2 * D_FF_EXPERT), D_MODEL),
        "moe_w_down": nrm(ks[25], (n_odd, N_EXPERTS, D_FF_EXPERT, D_MODEL), D_FF_EXPERT),
    }


def reference(x, norm_mix, norm_ffn, norm_out, rec_w_in, rec_conv_w, rec_conv_b,
              rec_gate_a_w, rec_gate_a_b, rec_gate_x_w, rec_gate_x_b, rec_lambda, rec_w_out,
              ffn_w_gate_up, ffn_w_down, cnv_w_pw1, cnv_b_pw1, cnv_dw_w, cnv_dw_b,
              cnv_ln_g, cnv_ln_b, cnv_w_pw2, cnv_b_pw2, moe_router, moe_w_gate_up, moe_w_down):
    h = x
    for layer in range(DEPTH):
        j = layer // 2
        if layer % 2 == 0:
            h = h + recurrent_block(rms_norm(h, norm_mix[layer]), rec_w_in[j], rec_conv_w[j],
                                    rec_conv_b[j], rec_gate_a_w[j], rec_gate_a_b[j],
                                    rec_gate_x_w[j], rec_gate_x_b[j], rec_lambda[j], rec_w_out[j])
            h = h + swiglu(rms_norm(h, norm_ffn[layer]), ffn_w_gate_up[j], ffn_w_down[j])
        else:
            h = h + conformer_conv_module(rms_norm(h, norm_mix[layer]), cnv_w_pw1[j], cnv_b_pw1[j],
                                          cnv_dw_w[j], cnv_dw_b[j], cnv_ln_g[j], cnv_ln_b[j],
                                          cnv_w_pw2[j], cnv_b_pw2[j])
            h = h + moe_swiglu(rms_norm(h, norm_ffn[layer]), moe_router[j], moe_w_gate_up[j],
                               moe_w_down[j])
    return rms_norm(h, norm_out)
```

```python
import functools
import math

import jax
import jax.numpy as jnp
from jax import lax
from jax.experimental import pallas as pl
from jax.experimental.pallas import tpu as pltpu

RMS_EPS = 1e-6
LN_EPS = 1e-5
RG_C = 8.0
TOP_K = 2

V7X_SUBLANES = 8
V7X_VMEM_BYTES = 64 * 1024 * 1024
VMEM_LIMIT_BYTES = V7X_VMEM_BYTES - 8 * 1024 * 1024

BF16 = jnp.bfloat16
F32 = jnp.float32


def _rms(x, g):
    return x * lax.rsqrt(jnp.mean(x * x, axis=-1, keepdims=True) + RMS_EPS) * g


def _sigmoid(x):
    return 1.0 / (1.0 + jnp.exp(-x))


def _silu(x):
    return x * _sigmoid(x)


def _neg_expm1(y):
    p = 1.0 + y * (1.0 / 6.0)
    p = 1.0 + y * (1.0 / 5.0) * p
    p = 1.0 + y * (1.0 / 4.0) * p
    p = 1.0 + y * (1.0 / 3.0) * p
    p = 1.0 + y * (1.0 / 2.0) * p
    return jnp.where(y > -0.125, -y * p, 1.0 - jnp.exp(y))


def _gelu_tanh(x):
    c = math.sqrt(2.0 / math.pi)
    return 0.5 * x * (1.0 + jnp.tanh(c * (x + 0.044715 * (x * x * x))))


def _params(*sem):
    return pltpu.CompilerParams(dimension_semantics=sem, vmem_limit_bytes=VMEM_LIMIT_BYTES)


def _const_spec(shape):
    return pl.BlockSpec(shape, lambda *_: (0,) * len(shape))


def _rec_kernel(x_ref, g_ref, win_ref, cw_ref, cb_ref, gw_ref, gab_ref, gxb_ref, lam_ref,
                wout_ref, o_ref, xbuf, a_s, u_s, hst, *, batch, heads):
    tm, d = x_ref.shape
    width = cw_ref.shape[0]
    halo = (width - 1) * batch
    blk = d // heads

    @pl.when(pl.program_id(0) == 0)
    def _():
        xbuf[0:halo, :] = jnp.zeros((halo, d), F32)
        hst[...] = jnp.zeros_like(hst)

    x = x_ref[...]
    xn = _rms(x, g_ref[...]).astype(BF16)
    proj = jnp.dot(xn, win_ref[...], preferred_element_type=F32)
    yb = _gelu_tanh(proj[:, :d])
    xbuf[halo:, :] = proj[:, d:]
    conv = cb_ref[...] + cw_ref[0:1, :] * xbuf[0:tm, :]
    for k in range(1, width):
        conv = conv + cw_ref[k:k + 1, :] * xbuf[k * batch:k * batch + tm, :]
    xbuf[0:halo, :] = xbuf[tm:tm + halo, :]

    z = -lam_ref[...]
    log_a_scale = -RG_C * (jnp.maximum(z, 0.0) + jnp.log1p(jnp.exp(-jnp.abs(z))))
    convb = conv.astype(BF16)
    for h in range(heads):
        hs = slice(h * blk, (h + 1) * blk)
        zz = jnp.dot(convb[:, hs], gw_ref[h], preferred_element_type=F32)
        r = _sigmoid(zz[:, :blk] + gab_ref[:, hs])
        i = _sigmoid(zz[:, blk:] + gxb_ref[:, hs])
        log_a = log_a_scale[:, hs] * r
        a_s[:, hs] = jnp.exp(log_a)
        u_s[:, hs] = jnp.sqrt(_neg_expm1(2.0 * log_a)) * (i * conv[:, hs])

    def step(t, h):
        r0 = pl.multiple_of(t * batch, batch)
        h = a_s[pl.ds(r0, batch), :] * h + u_s[pl.ds(r0, batch), :]
        u_s[pl.ds(r0, batch), :] = h
        return h

    hst[...] = lax.fori_loop(0, tm // batch, step, hst[...], unroll=8)
    mixed = (yb * u_s[...]).astype(BF16)
    o_ref[...] = x + jnp.dot(mixed, wout_ref[...], preferred_element_type=F32)


def _rec_block(h, g, w_in, conv_w, conv_b, gate_w, ga_b, gx_b, lam, w_out, *, batch, tm):
    t, d = h.shape
    heads = gate_w.shape[0]
    width = conv_w.shape[0]
    halo = (width - 1) * batch
    return pl.pallas_call(
        functools.partial(_rec_kernel, batch=batch, heads=heads),
        grid=(t // tm,),
        in_specs=[
            pl.BlockSpec((tm, d), lambda i: (i, 0)),
            _const_spec((1, d)),
            _const_spec(w_in.shape),
            _const_spec(conv_w.shape),
            _const_spec((1, d)),
            _const_spec(gate_w.shape),
            _const_spec((1, d)),
            _const_spec((1, d)),
            _const_spec((1, d)),
            _const_spec(w_out.shape),
        ],
        out_specs=pl.BlockSpec((tm, d), lambda i: (i, 0)),
        out_shape=jax.ShapeDtypeStruct((t, d), F32),
        scratch_shapes=[
            pltpu.VMEM((halo + tm, d), F32),
            pltpu.VMEM((tm, d), F32),
            pltpu.VMEM((tm, d), F32),
            pltpu.VMEM((batch, d), F32),
        ],
        compiler_params=_params("arbitrary"),
        name="rec_block",
    )(h, g, w_in, conv_w, conv_b, gate_w, ga_b, gx_b, lam, w_out)


def _cnv_kernel(x_ref, g_ref, w1_ref, b1_ref, dw_ref, db_ref, lg_ref, lb_ref, w2_ref, b2_ref,
                o_ref, cbuf, conv_s, *, batch, rows):
    tm, d = x_ref.shape
    width = dw_ref.shape[0]
    halo = (width - 1) * batch

    @pl.when(pl.program_id(0) == 0)
    def _():
        cbuf[0:halo, :] = jnp.zeros((halo, d), F32)

    x = x_ref[...]
    xn = _rms(x, g_ref[...]).astype(BF16)
    u = jnp.dot(xn, w1_ref[...], preferred_element_type=F32) + b1_ref[...]
    cbuf[halo:, :] = u[:, :d] * _sigmoid(u[:, d:])

    def chunk(c, carry):
        r0 = pl.multiple_of(c * rows, rows)
        acc = jnp.broadcast_to(db_ref[...], (rows, d))
        for k in range(width):
            acc = acc + dw_ref[k:k + 1, :] * cbuf[pl.ds(r0 + k * batch, rows), :]
        conv_s[pl.ds(r0, rows), :] = acc
        return carry

    lax.fori_loop(0, tm // rows, chunk, 0)
    cbuf[0:halo, :] = cbuf[tm:tm + halo, :]

    c = conv_s[...]
    mu = jnp.mean(c, axis=-1, keepdims=True)
    cc = c - mu
    var = jnp.mean(cc * cc, axis=-1, keepdims=True)
    y = _silu(cc * lax.rsqrt(var + LN_EPS) * lg_ref[...] + lb_ref[...])
    o_ref[...] = x + jnp.dot(y.astype(BF16), w2_ref[...], preferred_element_type=F32) + b2_ref[...]


def _cnv_block(h, g, w1, b1, dw, db, lg, lb, w2, b2, *, batch, tm):
    t, d = h.shape
    halo = (dw.shape[0] - 1) * batch
    assert tm >= halo
    return pl.pallas_call(
        functools.partial(_cnv_kernel, batch=batch, rows=4 * V7X_SUBLANES),
        grid=(t // tm,),
        in_specs=[
            pl.BlockSpec((tm, d), lambda i: (i, 0)),
            _const_spec((1, d)),
            _const_spec(w1.shape),
            _const_spec((1, 2 * d)),
            _const_spec(dw.shape),
            _const_spec((1, d)),
            _const_spec((1, d)),
            _const_spec((1, d)),
            _const_spec(w2.shape),
            _const_spec((1, d)),
        ],
        out_specs=pl.BlockSpec((tm, d), lambda i: (i, 0)),
        out_shape=jax.ShapeDtypeStruct((t, d), F32),
        scratch_shapes=[
            pltpu.VMEM((halo + tm, d), F32),
            pltpu.VMEM((tm, d), F32),
        ],
        compiler_params=_params("arbitrary"),
        name="cnv_block",
    )(h, g, w1, b1, dw, db, lg, lb, w2, b2)


def _router_kernel(x_ref, g_ref, wr_ref, gates_ref):
    xn = _rms(x_ref[...], g_ref[...])
    logits = jnp.dot(xn, wr_ref[...], preferred_element_type=F32, precision=lax.Precision.HIGHEST)
    n_exp = logits.shape[-1]
    lane = lax.broadcasted_iota(jnp.int32, logits.shape, 1)
    m1 = jnp.max(logits, axis=-1, keepdims=True)
    i1 = jnp.min(jnp.where(logits == m1, lane, n_exp), axis=-1, keepdims=True)
    rest = jnp.where(lane == i1, -jnp.inf, logits)
    m2 = jnp.max(rest, axis=-1, keepdims=True)
    i2 = jnp.min(jnp.where(rest == m2, lane, n_exp), axis=-1, keepdims=True)
    e2 = jnp.exp(m2 - m1)
    denom = 1.0 + e2
    gates_ref[...] = jnp.where(lane == i1, 1.0 / denom, 0.0) + jnp.where(lane == i2, e2 / denom, 0.0)


def _router(h, g, w_router, *, tm):
    t, d = h.shape
    n_exp = w_router.shape[1]
    return pl.pallas_call(
        _router_kernel,
        grid=(t // tm,),
        in_specs=[
            pl.BlockSpec((tm, d), lambda i: (i, 0)),
            _const_spec((1, d)),
            _const_spec(w_router.shape),
        ],
        out_specs=pl.BlockSpec((tm, n_exp), lambda i: (i, 0)),
        out_shape=jax.ShapeDtypeStruct((t, n_exp), F32),
        compiler_params=_params("parallel"),
        name="router",
    )(h, g, w_router)


def _ffn_kernel(x_ref, g_ref, gates_ref, wg_ref, wu_ref, wd_ref, o_ref, xn_s, acc_s, gate_s):
    e = pl.program_id(1)
    j = pl.program_id(2)

    @pl.when((e == 0) & (j == 0))
    def _():
        xn_s[...] = _rms(x_ref[...], g_ref[...]).astype(BF16)
        acc_s[...] = jnp.zeros_like(acc_s)

    @pl.when(j == 0)
    def _():
        gates = gates_ref[...]
        lane = lax.broadcasted_iota(jnp.int32, gates.shape, 1)
        gate_s[...] = jnp.sum(jnp.where(lane == e, gates, 0.0), axis=-1, keepdims=True)

    xn = xn_s[...]
    gg = jnp.dot(xn, wg_ref[0], preferred_element_type=F32)
    uu = jnp.dot(xn, wu_ref[0], preferred_element_type=F32)
    act = (_silu(gg) * uu * gate_s[...]).astype(BF16)
    acc_s[...] += jnp.dot(act, wd_ref[0], preferred_element_type=F32)

    @pl.when((e == pl.num_programs(1) - 1) & (j == pl.num_programs(2) - 1))
    def _():
        o_ref[...] = x_ref[...] + acc_s[...]


def _ffn(h, g, gates, w_gu, w_d, *, tm, tf):
    t, d = h.shape
    n_exp, _, f2 = w_gu.shape
    f = f2 // 2
    nf = f // tf
    return pl.pallas_call(
        _ffn_kernel,
        grid=(t // tm, n_exp, nf),
        in_specs=[
            pl.BlockSpec((tm, d), lambda i, e, j: (i, 0)),
            _const_spec((1, d)),
            pl.BlockSpec((tm, n_exp), lambda i, e, j: (i, 0)),
            pl.BlockSpec((1, d, tf), lambda i, e, j: (e, 0, j)),
            pl.BlockSpec((1, d, tf), lambda i, e, j: (e, 0, j + nf)),
            pl.BlockSpec((1, tf, d), lambda i, e, j: (e, j, 0)),
        ],
        out_specs=pl.BlockSpec((tm, d), lambda i, e, j: (i, 0)),
        out_shape=jax.ShapeDtypeStruct((t, d), F32),
        scratch_shapes=[
            pltpu.VMEM((tm, d), BF16),
            pltpu.VMEM((tm, d), F32),
            pltpu.VMEM((tm, 1), F32),
        ],
        compiler_params=_params("parallel", "arbitrary", "arbitrary"),
        name="ffn",
    )(h, g, gates, w_gu, w_gu, w_d)


def _final_norm_kernel(x_ref, g_ref, o_ref):
    o_ref[...] = _rms(x_ref[...], g_ref[...])


def _final_norm(h, g, *, tm):
    t, d = h.shape
    return pl.pallas_call(
        _final_norm_kernel,
        grid=(t // tm,),
        in_specs=[pl.BlockSpec((tm, d), lambda i: (i, 0)), _const_spec((1, d))],
        out_specs=pl.BlockSpec((tm, d), lambda i: (i, 0)),
        out_shape=jax.ShapeDtypeStruct((t, d), F32),
        compiler_params=_params("parallel"),
        name="final_norm",
    )(h, g)


def _pick(n, want):
    while n % want:
        want //= 2
    return want


def kernel(x, norm_mix, norm_ffn, norm_out, rec_w_in, rec_conv_w, rec_conv_b, rec_gate_a_w, rec_gate_a_b, rec_gate_x_w, rec_gate_x_b, rec_lambda, rec_w_out, ffn_w_gate_up, ffn_w_down, cnv_w_pw1, cnv_b_pw1, cnv_dw_w, cnv_dw_b, cnv_ln_g, cnv_ln_b, cnv_w_pw2, cnv_b_pw2, moe_router, moe_w_gate_up, moe_w_down):
    bsz, seq, d = x.shape
    assert bsz == V7X_SUBLANES, "time-major layout puts the batch on the sublane axis"
    t = bsz * seq
    depth = norm_mix.shape[0]
    tm_mix = _pick(t, 512)
    tm_ffn = _pick(t, 1024)
    tf = _pick(ffn_w_down.shape[1], 512)
    tf_moe = _pick(moe_w_down.shape[2], 512)

    h = jnp.transpose(x, (1, 0, 2)).reshape(t, d)
    row = lambda v: v.reshape(1, -1)
    for layer in range(depth):
        j = layer // 2
        if layer % 2 == 0:
            gate_w = jnp.concatenate([rec_gate_a_w[j], rec_gate_x_w[j]], axis=-1).astype(BF16)
            h = _rec_block(h, row(norm_mix[layer]), rec_w_in[j].astype(BF16), rec_conv_w[j],
                           row(rec_conv_b[j]), gate_w, row(rec_gate_a_b[j]), row(rec_gate_x_b[j]),
                           row(rec_lambda[j]), rec_w_out[j].astype(BF16), batch=bsz, tm=tm_mix)
            ones = jnp.ones((t, 1), F32)
            h = _ffn(h, row(norm_ffn[layer]), ones, ffn_w_gate_up[j][None].astype(BF16),
                     ffn_w_down[j][None].astype(BF16), tm=tm_ffn, tf=tf)
        else:
            h = _cnv_block(h, row(norm_mix[layer]), cnv_w_pw1[j].astype(BF16), row(cnv_b_pw1[j]),
                           cnv_dw_w[j], row(cnv_dw_b[j]), row(cnv_ln_g[j]), row(cnv_ln_b[j]),
                           cnv_w_pw2[j].astype(BF16), row(cnv_b_pw2[j]), batch=bsz, tm=tm_mix)
            gates = _router(h, row(norm_ffn[layer]), moe_router[j], tm=tm_ffn)
            h = _ffn(h, row(norm_ffn[layer]), gates, moe_w_gate_up[j].astype(BF16),
                     moe_w_down[j].astype(BF16), tm=tm_ffn, tf=tf_moe)
    out = _final_norm(h, row(norm_out), tm=tm_ffn)
    return jnp.transpose(out.reshape(seq, bsz, d), (1, 0, 2))
```

```python
import functools
import math

import jax
import jax.numpy as jnp
from jax import lax
from jax.experimental import pallas as pl
from jax.experimental.pallas import tpu as pltpu

RMS_EPS = 1e-6
LN_EPS = 1e-5
RG_C = 8.0
TOP_K = 2

V7X_SUBLANES = 8
V7X_VMEM_BYTES = 64 * 1024 * 1024
VMEM_LIMIT_BYTES = V7X_VMEM_BYTES - 8 * 1024 * 1024

BF16 = jnp.bfloat16
F32 = jnp.float32


def _rms(x, g):
    return x * lax.rsqrt(jnp.mean(x * x, axis=-1, keepdims=True) + RMS_EPS) * g


def _sigmoid(x):
    return 1.0 / (1.0 + jnp.exp(-x))


def _silu(x):
    return x * _sigmoid(x)


def _neg_expm1(y):
    p = 1.0 + y * (1.0 / 6.0)
    p = 1.0 + y * (1.0 / 5.0) * p
    p = 1.0 + y * (1.0 / 4.0) * p
    p = 1.0 + y * (1.0 / 3.0) * p
    p = 1.0 + y * (1.0 / 2.0) * p
    return jnp.where(y > -0.125, -y * p, 1.0 - jnp.exp(y))


def _gelu_tanh(x):
    c = math.sqrt(2.0 / math.pi)
    return 0.5 * x * (1.0 + jnp.tanh(c * (x + 0.044715 * (x * x * x))))


def _params(*sem):
    return pltpu.CompilerParams(dimension_semantics=sem, vmem_limit_bytes=VMEM_LIMIT_BYTES)


def _const_spec(shape):
    return pl.BlockSpec(shape, lambda *_: (0,) * len(shape))


def _rec_kernel(x_ref, g_ref, win_ref, cw_ref, cb_ref, gw_ref, gab_ref, gxb_ref, lam_ref,
                wout_ref, o_ref, xbuf, a_s, u_s, hst, *, batch, heads):
    tm, d = x_ref.shape
    width = cw_ref.shape[0]
    halo = (width - 1) * batch
    blk = d // heads

    @pl.when(pl.program_id(0) == 0)
    def _():
        xbuf[0:halo, :] = jnp.zeros((halo, d), F32)
        hst[...] = jnp.zeros_like(hst)

    x = x_ref[...]
    xn = _rms(x, g_ref[...]).astype(BF16)
    proj = jnp.dot(xn, win_ref[...], preferred_element_type=F32)
    yb = _gelu_tanh(proj[:, :d])
    xbuf[halo:, :] = proj[:, d:]
    conv = cb_ref[...] + cw_ref[0:1, :] * xbuf[0:tm, :]
    for k in range(1, width):
        conv = conv + cw_ref[k:k + 1, :] * xbuf[k * batch:k * batch + tm, :]
    xbuf[0:halo, :] = xbuf[tm:tm + halo, :]

    z = -lam_ref[...]
    log_a_scale = -RG_C * (jnp.maximum(z, 0.0) + jnp.log1p(jnp.exp(-jnp.abs(z))))
    convb = conv.astype(BF16)
    for h in range(heads):
        hs = slice(h * blk, (h + 1) * blk)
        zz = jnp.dot(convb[:, hs], gw_ref[h], preferred_element_type=F32)
        r = _sigmoid(zz[:, :blk] + gab_ref[:, hs])
        i = _sigmoid(zz[:, blk:] + gxb_ref[:, hs])
        log_a = log_a_scale[:, hs] * r
        a_s[:, hs] = jnp.exp(log_a)
        u_s[:, hs] = jnp.sqrt(_neg_expm1(2.0 * log_a)) * (i * conv[:, hs])

    def step(t, h):
        r0 = pl.multiple_of(t * batch, batch)
        h = a_s[pl.ds(r0, batch), :] * h + u_s[pl.ds(r0, batch), :]
        u_s[pl.ds(r0, batch), :] = h
        return h

    hst[...] = lax.fori_loop(0, tm // batch, step, hst[...], unroll=8)
    mixed = (yb * u_s[...]).astype(BF16)
    o_ref[...] = x + jnp.dot(mixed, wout_ref[...], preferred_element_type=F32)


def _rec_block(h, g, w_in, conv_w, conv_b, gate_w, ga_b, gx_b, lam, w_out, *, batch, tm):
    t, d = h.shape
    heads = gate_w.shape[0]
    width = conv_w.shape[0]
    halo = (width - 1) * batch
    return pl.pallas_call(
        functools.partial(_rec_kernel, batch=batch, heads=heads),
        grid=(t // tm,),
        in_specs=[
            pl.BlockSpec((tm, d), lambda i: (i, 0)),
            _const_spec((1, d)),
            _const_spec(w_in.shape),
            _const_spec(conv_w.shape),
            _const_spec((1, d)),
            _const_spec(gate_w.shape),
            _const_spec((1, d)),
            _const_spec((1, d)),
            _const_spec((1, d)),
            _const_spec(w_out.shape),
        ],
        out_specs=pl.BlockSpec((tm, d), lambda i: (i, 0)),
        out_shape=jax.ShapeDtypeStruct((t, d), F32),
        scratch_shapes=[
            pltpu.VMEM((halo + tm, d), F32),
            pltpu.VMEM((tm, d), F32),
            pltpu.VMEM((tm, d), F32),
            pltpu.VMEM((batch, d), F32),
        ],
        compiler_params=_params("arbitrary"),
        name="rec_block",
    )(h, g, w_in, conv_w, conv_b, gate_w, ga_b, gx_b, lam, w_out)


def _cnv_kernel(x_ref, g_ref, w1_ref, b1_ref, dw_ref, db_ref, lg_ref, lb_ref, w2_ref, b2_ref,
                o_ref, cbuf, conv_s, *, batch, rows):
    tm, d = x_ref.shape
    width = dw_ref.shape[0]
    halo = (width - 1) * batch

    @pl.when(pl.program_id(0) == 0)
    def _():
        cbuf[0:halo, :] = jnp.zeros((halo, d), F32)

    x = x_ref[...]
    xn = _rms(x, g_ref[...]).astype(BF16)
    u = jnp.dot(xn, w1_ref[...], preferred_element_type=F32) + b1_ref[...]
    cbuf[halo:, :] = u[:, :d] * _sigmoid(u[:, d:])

    def chunk(c, carry):
        r0 = pl.multiple_of(c * rows, rows)
        acc = jnp.broadcast_to(db_ref[...], (rows, d))
        for k in range(width):
            acc = acc + dw_ref[k:k + 1, :] * cbuf[pl.ds(r0 + k * batch, rows), :]
        conv_s[pl.ds(r0, rows), :] = acc
        return carry

    lax.fori_loop(0, tm // rows, chunk, 0)
    cbuf[0:halo, :] = cbuf[tm:tm + halo, :]

    c = conv_s[...]
    mu = jnp.mean(c, axis=-1, keepdims=True)
    cc = c - mu
    var = jnp.mean(cc * cc, axis=-1, keepdims=True)
    y = _silu(cc * lax.rsqrt(var + LN_EPS) * lg_ref[...] + lb_ref[...])
    o_ref[...] = x + jnp.dot(y.astype(BF16), w2_ref[...], preferred_element_type=F32) + b2_ref[...]


def _cnv_block(h, g, w1, b1, dw, db, lg, lb, w2, b2, *, batch, tm):
    t, d = h.shape
    halo = (dw.shape[0] - 1) * batch
    assert tm >= halo
    return pl.pallas_call(
        functools.partial(_cnv_kernel, batch=batch, rows=4 * V7X_SUBLANES),
        grid=(t // tm,),
        in_specs=[
            pl.BlockSpec((tm, d), lambda i: (i, 0)),
            _const_spec((1, d)),
            _const_spec(w1.shape),
            _const_spec((1, 2 * d)),
            _const_spec(dw.shape),
            _const_spec((1, d)),
            _const_spec((1, d)),
            _const_spec((1, d)),
            _const_spec(w2.shape),
            _const_spec((1, d)),
        ],
        out_specs=pl.BlockSpec((tm, d), lambda i: (i, 0)),
        out_shape=jax.ShapeDtypeStruct((t, d), F32),
        scratch_shapes=[
            pltpu.VMEM((halo + tm, d), F32),
            pltpu.VMEM((tm, d), F32),
        ],
        compiler_params=_params("arbitrary"),
        name="cnv_block",
    )(h, g, w1, b1, dw, db, lg, lb, w2, b2)


def _router_kernel(x_ref, g_ref, wr_ref, xn_ref, idx_ref, w_ref):
    xn = _rms(x_ref[...], g_ref[...])
    xn_ref[...] = xn
    logits = jnp.dot(xn, wr_ref[...], preferred_element_type=F32, precision=lax.Precision.HIGHEST)
    n_exp = logits.shape[-1]
    lane = lax.broadcasted_iota(jnp.int32, logits.shape, 1)
    m1 = jnp.max(logits, axis=-1, keepdims=True)
    i1 = jnp.min(jnp.where(logits == m1, lane, n_exp), axis=-1, keepdims=True)
    rest = jnp.where(lane == i1, -jnp.inf, logits)
    m2 = jnp.max(rest, axis=-1, keepdims=True)
    i2 = jnp.min(jnp.where(rest == m2, lane, n_exp), axis=-1, keepdims=True)
    e2 = jnp.exp(m2 - m1)
    denom = 1.0 + e2
    first = lax.broadcasted_iota(jnp.int32, idx_ref.shape, 1) == 0
    idx_ref[...] = jnp.where(first, i1, i2)
    w_ref[...] = jnp.where(first, 1.0 / denom, e2 / denom)


def _router(h, g, w_router, *, tm):
    t, d = h.shape
    return pl.pallas_call(
        _router_kernel,
        grid=(t // tm,),
        in_specs=[
            pl.BlockSpec((tm, d), lambda i: (i, 0)),
            _const_spec((1, d)),
            _const_spec(w_router.shape),
        ],
        out_specs=[
            pl.BlockSpec((tm, d), lambda i: (i, 0)),
            pl.BlockSpec((tm, TOP_K), lambda i: (i, 0)),
            pl.BlockSpec((tm, TOP_K), lambda i: (i, 0)),
        ],
        out_shape=[
            jax.ShapeDtypeStruct((t, d), F32),
            jax.ShapeDtypeStruct((t, TOP_K), jnp.int32),
            jax.ShapeDtypeStruct((t, TOP_K), F32),
        ],
        compiler_params=_params("parallel"),
        name="router",
    )(h, g, w_router)


def _dispatch_plan(idx, n_exp, tm):
    t, k = idx.shape
    n_assign = t * k
    n_tiles = n_assign // tm + n_exp
    flat_e = idx.reshape(n_assign)
    onehot = (flat_e[:, None] == jnp.arange(n_exp, dtype=jnp.int32)[None, :]).astype(jnp.int32)
    csum = jnp.cumsum(onehot, axis=0)
    counts = csum[-1]
    rank = jnp.sum((csum - onehot) * onehot, axis=1)
    tiles_per_e = (counts + tm - 1) // tm
    tile_end = jnp.cumsum(tiles_per_e)
    tile_start = tile_end - tiles_per_e
    slot = tile_start[flat_e] * tm + rank
    assign = jnp.arange(n_assign, dtype=jnp.int32)
    dst_row = (assign % k) * t + assign // k

    tile = jnp.arange(n_tiles, dtype=jnp.int32)
    n_used = tile_end[-1]
    tile_e = jnp.minimum(jnp.searchsorted(tile_end, jnp.minimum(tile, n_used - 1), side="right"),
                         n_exp - 1).astype(jnp.int32)
    tile_nv = jnp.where(tile < n_used,
                        jnp.clip(counts[tile_e] - (tile - tile_start[tile_e]) * tm, 0, tm), 0).astype(jnp.int32)
    slot_dst = jnp.zeros((n_tiles * tm,), jnp.int32).at[slot].set(dst_row)
    return tile_e, tile_nv, slot_dst


def _moe_kernel(te_ref, nv_ref, dst_ref, xn_hbm, wg_ref, wu_ref, wd_ref, out_hbm,
                xbuf, obuf, xb16, acc_s, gsem, ssem, *, n_tok, part):
    del te_ref
    i = pl.program_id(0)
    j = pl.program_id(1)
    nt = pl.num_programs(0)
    nf = pl.num_programs(1)
    tm = xb16.shape[0]

    def used(tile):
        return nv_ref[tile] > 0

    def issue_gather(tile, lo, hi):
        b = tile % 2

        def body(r, c):
            d = dst_ref[tile * tm + r]
            tok = d - jnp.where(d >= n_tok, n_tok, 0)
            pltpu.make_async_copy(xn_hbm.at[pl.ds(tok, 1), :], xbuf.at[b, pl.ds(r, 1), :], gsem.at[b]).start()
            return c

        lax.fori_loop(lo, hi, body, 0)

    def issue_scatter(tile, lo, hi):
        b = tile % 2

        def body(r, c):
            d = dst_ref[tile * tm + r]
            pltpu.make_async_copy(obuf.at[b, pl.ds(r, 1), :], out_hbm.at[pl.ds(d, 1), :], ssem.at[b]).start()
            return c

        lax.fori_loop(lo, jnp.minimum(hi, nv_ref[tile]), body, 0)

    def wait_gather(tile):
        b = tile % 2
        pltpu.make_async_copy(xn_hbm.at[pl.ds(0, tm), :], xbuf.at[b], gsem.at[b]).wait()

    def wait_scatter(tile):
        b = tile % 2
        n = nv_ref[tile]
        n_al = pl.multiple_of((n // V7X_SUBLANES) * V7X_SUBLANES, V7X_SUBLANES)
        pltpu.make_async_copy(obuf.at[b, pl.ds(0, n_al), :], out_hbm.at[pl.ds(0, n_al), :], ssem.at[b]).wait()

        def one_row(r, c):
            pltpu.make_async_copy(obuf.at[b, pl.ds(0, 1), :], out_hbm.at[pl.ds(0, 1), :], ssem.at[b]).wait()
            return c

        lax.fori_loop(n_al, n, one_row, 0)

    lo = j * part
    hi = jnp.minimum(lo + part, tm)
    nxt = jnp.minimum(i + 1, nt - 1)
    prv = jnp.maximum(i - 1, 0)
    prv2 = jnp.maximum(i - 2, 0)

    @pl.when((i == 0) & (j == 0) & used(0))
    def _():
        issue_gather(0, 0, tm)

    @pl.when((j == 0) & used(i))
    def _():
        wait_gather(i)
        xb16[...] = xbuf[i % 2].astype(BF16)

    @pl.when((i + 1 < nt) & used(nxt))
    def _():
        issue_gather(i + 1, lo, hi)

    @pl.when((i >= 1) & used(prv))
    def _():
        issue_scatter(i - 1, lo, hi)

    @pl.when(used(i))
    def _():
        xv = xb16[...]
        gg = jnp.dot(xv, wg_ref[0], preferred_element_type=F32)
        uu = jnp.dot(xv, wu_ref[0], preferred_element_type=F32)
        act = (_silu(gg) * uu).astype(BF16)
        part_out = jnp.dot(act, wd_ref[0], preferred_element_type=F32)

        @pl.when(j == 0)
        def _():
            acc_s[...] = part_out

        @pl.when(j > 0)
        def _():
            acc_s[...] += part_out

    @pl.when(j == nf - 1)
    def _():
        @pl.when((i >= 2) & used(prv2))
        def _():
            wait_scatter(i - 2)

        @pl.when(used(i))
        def _():
            obuf[i % 2] = acc_s[...]

        @pl.when(i == nt - 1)
        def _():
            @pl.when(used(i))
            def _():
                issue_scatter(i, 0, tm)

            @pl.when((i >= 1) & used(prv))
            def _():
                wait_scatter(i - 1)

            @pl.when(used(i))
            def _():
                wait_scatter(i)


def _moe_routed(xn, tile_e, tile_nv, slot_dst, w_gu, w_d, *, tm, tf):
    t, d = xn.shape
    n_exp, _, f2 = w_gu.shape
    nf = (f2 // 2) // tf
    n_tiles = tile_e.shape[0]
    part = -(-tm // nf)

    def wmap(col0):
        def index_map(i, j, te, nv, dst):
            del dst
            return te[i], 0, col0 + jnp.where(nv[i] > 0, j, nf - 1)
        return index_map

    def dmap(i, j, te, nv, dst):
        del dst
        return te[i], jnp.where(nv[i] > 0, j, nf - 1), 0

    grid_spec = pltpu.PrefetchScalarGridSpec(
        num_scalar_prefetch=3,
        grid=(n_tiles, nf),
        in_specs=[
            pl.BlockSpec(memory_space=pl.ANY),
            pl.BlockSpec((1, d, tf), wmap(0)),
            pl.BlockSpec((1, d, tf), wmap(nf)),
            pl.BlockSpec((1, tf, d), dmap),
        ],
        out_specs=pl.BlockSpec(memory_space=pl.ANY),
        scratch_shapes=[
            pltpu.VMEM((2, tm, d), F32),
            pltpu.VMEM((2, tm, d), F32),
            pltpu.VMEM((tm, d), BF16),
            pltpu.VMEM((tm, d), F32),
            pltpu.SemaphoreType.DMA((2,)),
            pltpu.SemaphoreType.DMA((2,)),
        ],
    )
    return pl.pallas_call(
        functools.partial(_moe_kernel, n_tok=t, part=part),
        grid_spec=grid_spec,
        out_shape=jax.ShapeDtypeStruct((TOP_K * t, d), F32),
        compiler_params=_params("arbitrary", "arbitrary"),
        name="moe_experts",
    )(tile_e, tile_nv, slot_dst, xn, w_gu, w_gu, w_d)


def _combine_kernel(h_ref, w_ref, o1_ref, o2_ref, out_ref):
    w = w_ref[...]
    out_ref[...] = h_ref[...] + w[:, 0:1] * o1_ref[...] + w[:, 1:2] * o2_ref[...]


def _combine(h, w, eo, *, tm):
    t, d = h.shape
    nb = t // tm
    return pl.pallas_call(
        _combine_kernel,
        grid=(nb,),
        in_specs=[
            pl.BlockSpec((tm, d), lambda i: (i, 0)),
            pl.BlockSpec((tm, TOP_K), lambda i: (i, 0)),
            pl.BlockSpec((tm, d), lambda i: (i, 0)),
            pl.BlockSpec((tm, d), lambda i: (i + nb, 0)),
        ],
        out_specs=pl.BlockSpec((tm, d), lambda i: (i, 0)),
        out_shape=jax.ShapeDtypeStruct((t, d), F32),
        compiler_params=_params("parallel"),
        name="moe_combine",
    )(h, w, eo, eo)


def _ffn_kernel(x_ref, g_ref, wg_ref, wu_ref, wd_ref, o_ref, xn_s, acc_s):
    j = pl.program_id(1)

    @pl.when(j == 0)
    def _():
        xn_s[...] = _rms(x_ref[...], g_ref[...]).astype(BF16)
        acc_s[...] = jnp.zeros_like(acc_s)

    xn = xn_s[...]
    gg = jnp.dot(xn, wg_ref[...], preferred_element_type=F32)
    uu = jnp.dot(xn, wu_ref[...], preferred_element_type=F32)
    act = (_silu(gg) * uu).astype(BF16)
    acc_s[...] += jnp.dot(act, wd_ref[...], preferred_element_type=F32)

    @pl.when(j == pl.num_programs(1) - 1)
    def _():
        o_ref[...] = x_ref[...] + acc_s[...]


def _ffn(h, g, w_gu, w_d, *, tm, tf):
    t, d = h.shape
    nf = (w_gu.shape[1] // 2) // tf
    return pl.pallas_call(
        _ffn_kernel,
        grid=(t // tm, nf),
        in_specs=[
            pl.BlockSpec((tm, d), lambda i, j: (i, 0)),
            _const_spec((1, d)),
            pl.BlockSpec((d, tf), lambda i, j: (0, j)),
            pl.BlockSpec((d, tf), lambda i, j: (0, j + nf)),
            pl.BlockSpec((tf, d), lambda i, j: (j, 0)),
        ],
        out_specs=pl.BlockSpec((tm, d), lambda i, j: (i, 0)),
        out_shape=jax.ShapeDtypeStruct((t, d), F32),
        scratch_shapes=[
            pltpu.VMEM((tm, d), BF16),
            pltpu.VMEM((tm, d), F32),
        ],
        compiler_params=_params("parallel", "arbitrary"),
        name="ffn",
    )(h, g, w_gu, w_gu, w_d)


def _final_norm_kernel(x_ref, g_ref, o_ref):
    o_ref[...] = _rms(x_ref[...], g_ref[...])


def _final_norm(h, g, *, tm):
    t, d = h.shape
    return pl.pallas_call(
        _final_norm_kernel,
        grid=(t // tm,),
        in_specs=[pl.BlockSpec((tm, d), lambda i: (i, 0)), _const_spec((1, d))],
        out_specs=pl.BlockSpec((tm, d), lambda i: (i, 0)),
        out_shape=jax.ShapeDtypeStruct((t, d), F32),
        compiler_params=_params("parallel"),
        name="final_norm",
    )(h, g)


def _pick(n, want):
    while n % want:
        want //= 2
    return want


def kernel(x, norm_mix, norm_ffn, norm_out, rec_w_in, rec_conv_w, rec_conv_b, rec_gate_a_w, rec_gate_a_b, rec_gate_x_w, rec_gate_x_b, rec_lambda, rec_w_out, ffn_w_gate_up, ffn_w_down, cnv_w_pw1, cnv_b_pw1, cnv_dw_w, cnv_dw_b, cnv_ln_g, cnv_ln_b, cnv_w_pw2, cnv_b_pw2, moe_router, moe_w_gate_up, moe_w_down):
    bsz, seq, d = x.shape
    assert bsz == V7X_SUBLANES, "time-major layout puts the batch on the sublane axis"
    t = bsz * seq
    depth = norm_mix.shape[0]
    n_exp = moe_router.shape[-1]
    tm_mix = _pick(t, 512)
    tm_ffn = _pick(t, 1024)
    tm_moe = _pick(t * TOP_K, 1024)
    tf = _pick(ffn_w_down.shape[1], 512)
    tf_moe = _pick(moe_w_down.shape[2], 512)

    h = jnp.transpose(x, (1, 0, 2)).reshape(t, d)
    row = lambda v: v.reshape(1, -1)
    for layer in range(depth):
        j = layer // 2
        if layer % 2 == 0:
            gate_w = jnp.concatenate([rec_gate_a_w[j], rec_gate_x_w[j]], axis=-1).astype(BF16)
            h = _rec_block(h, row(norm_mix[layer]), rec_w_in[j].astype(BF16), rec_conv_w[j],
                           row(rec_conv_b[j]), gate_w, row(rec_gate_a_b[j]), row(rec_gate_x_b[j]),
                           row(rec_lambda[j]), rec_w_out[j].astype(BF16), batch=bsz, tm=tm_mix)
            h = _ffn(h, row(norm_ffn[layer]), ffn_w_gate_up[j].astype(BF16),
                     ffn_w_down[j].astype(BF16), tm=tm_ffn, tf=tf)
        else:
            h = _cnv_block(h, row(norm_mix[layer]), cnv_w_pw1[j].astype(BF16), row(cnv_b_pw1[j]),
                           cnv_dw_w[j], row(cnv_dw_b[j]), row(cnv_ln_g[j]), row(cnv_ln_b[j]),
                           cnv_w_pw2[j].astype(BF16), row(cnv_b_pw2[j]), batch=bsz, tm=tm_mix)
            xn, idx, topw = _router(h, row(norm_ffn[layer]), moe_router[j], tm=tm_ffn)
            tile_e, tile_nv, slot_dst = _dispatch_plan(idx, n_exp, tm_moe)
            eo = _moe_routed(xn, tile_e, tile_nv, slot_dst, moe_w_gate_up[j].astype(BF16),
                             moe_w_down[j].astype(BF16), tm=tm_moe, tf=tf_moe)
            h = _combine(h, topw, eo, tm=tm_ffn)
    out = _final_norm(h, row(norm_out), tm=tm_ffn)
    return jnp.transpose(out.reshape(seq, bsz, d), (1, 0, 2))
```
